```python
import math
import jax, jax.numpy as jnp
from jax import lax
import numpy as np

D_MODEL = 2048
BATCH = 4
SEQ = 2048
DEPTH = 1

HEAD_DIM = 64
N_Q_HEADS = 16
N_KV_HEADS = 2
GQA_GROUP = N_Q_HEADS // N_KV_HEADS
ATTN_WIDTH = N_Q_HEADS * HEAD_DIM
KV_WIDTH = N_KV_HEADS * HEAD_DIM
WINDOW = 128
BLOCK = 128
MIX_WIDTH = D_MODEL
CONV_WIDTH = MIX_WIDTH - ATTN_WIDTH
CONV_GROUP_SIZE = 64
N_CONV_GROUPS = CONV_WIDTH // CONV_GROUP_SIZE
CONV_KERNEL = 31
IN_WIDTH = ATTN_WIDTH + 2 * KV_WIDTH + 2 * CONV_WIDTH
D_FF = 5632
LN_EPS = 1e-5
DEEPNORM_ALPHA = (2.0 * DEPTH) ** 0.25
DEEPNORM_BETA = (8.0 * DEPTH) ** -0.25
ATTN_SCALE = 1.0 / math.sqrt(HEAD_DIM)

kernel_name = "hymba_swa_sink_conformer_conv_macaron_deepnorm"


def layer_norm(x, g, b):
    xf = x.astype(jnp.float32)
    mu = jnp.mean(xf, axis=-1, keepdims=True)
    var = jnp.mean(jnp.square(xf - mu), axis=-1, keepdims=True)
    y = (xf - mu) * lax.rsqrt(var + LN_EPS) * g.astype(jnp.float32) + b.astype(jnp.float32)
    return y.astype(x.dtype)


def swiglu_ffn(x, w_gate, w_up, w_down):
    h = jax.nn.silu(x @ w_gate) * (x @ w_up)
    return h @ w_down


def sliding_window_attention_with_sinks(q, k, v, sinks):
    B, S = q.shape[0], q.shape[1]
    nb = S // BLOCK
    qb = q.reshape(B, nb, BLOCK, N_KV_HEADS, GQA_GROUP, HEAD_DIM)
    pad = ((0, 0), (BLOCK, 0), (0, 0), (0, 0))
    kp = jnp.pad(k, pad).reshape(B, nb + 1, BLOCK, N_KV_HEADS, HEAD_DIM)
    vp = jnp.pad(v, pad).reshape(B, nb + 1, BLOCK, N_KV_HEADS, HEAD_DIM)
    kb = jnp.concatenate([kp[:, :-1], kp[:, 1:]], axis=2)
    vb = jnp.concatenate([vp[:, :-1], vp[:, 1:]], axis=2)
    scores = jnp.einsum('bnqhgd,bnkhd->bnhgqk', qb, kb).astype(jnp.float32) * ATTN_SCALE
    q_rel = jnp.arange(BLOCK)[:, None] + BLOCK
    k_rel = jnp.arange(2 * BLOCK)[None, :]
    delta = q_rel - k_rel
    band = (delta >= 0) & (delta < WINDOW)
    k_abs = jnp.arange(nb)[:, None] * BLOCK - BLOCK + k_rel
    valid = band[None] & (k_abs >= 0)[:, None, :]
    scores = jnp.where(valid[None, :, None, None], scores, jnp.finfo(jnp.float32).min)
    sink = sinks.astype(jnp.float32).reshape(N_KV_HEADS, GQA_GROUP)[None, None, :, :, None, None]
    sink = jnp.broadcast_to(sink, scores.shape[:-1] + (1,))
    probs = jax.nn.softmax(jnp.concatenate([scores, sink], axis=-1), axis=-1)[..., :-1]
    out = jnp.einsum('bnhgqk,bnkhd->bnqhgd', probs.astype(v.dtype), vb)
    return out.reshape(B, S, ATTN_WIDTH)


def conformer_conv_group(u, dw_w, dw_b, ln_g, ln_b):
    a, gate = jnp.split(u, 2, axis=-1)
    h = a * jax.nn.sigmoid(gate)
    h = lax.conv_general_dilated(
        h, dw_w[:, None, :], window_strides=(1,), padding=[(CONV_KERNEL - 1, 0)],
        dimension_numbers=('NWC', 'WIO', 'NWC'), feature_group_count=CONV_WIDTH) + dw_b
    h = layer_norm(h, ln_g, ln_b)
    return jax.nn.silu(h)


def setup_inputs(seed: int = 0) -> dict:
    key = jax.random.key(seed)
    ks = jax.random.split(key, 24)
    L = DEPTH
    nrm = lambda k, shape, s: jax.random.normal(k, shape, jnp.float32) * s
    gain = lambda k, n: 1.0 + nrm(k, (L, n), 0.02)
    return {
        "x": nrm(ks[0], (BATCH, SEQ, D_MODEL), 1.0),
        "ffn1_w_gate": nrm(ks[1], (L, D_MODEL, D_FF), D_MODEL ** -0.5),
        "ffn1_w_up": nrm(ks[2], (L, D_MODEL, D_FF), D_MODEL ** -0.5),
        "ffn1_w_down": nrm(ks[3], (L, D_FF, D_MODEL), D_FF ** -0.5 * DEEPNORM_BETA),
        "ln1_g": gain(ks[4], D_MODEL),
        "ln1_b": nrm(ks[5], (L, D_MODEL), 0.02),
        "w_in": nrm(ks[6], (L, D_MODEL, IN_WIDTH), D_MODEL ** -0.5),
        "b_in": nrm(ks[7], (L, IN_WIDTH), 0.02),
        "attn_sinks": nrm(ks[8], (L, N_Q_HEADS), 0.5),
        "conv_dw_w": nrm(ks[9], (L, CONV_KERNEL, CONV_WIDTH), CONV_KERNEL ** -0.5),
        "conv_dw_b": nrm(ks[10], (L, CONV_WIDTH), 0.02),
        "conv_ln_g": gain(ks[11], CONV_WIDTH),
        "conv_ln_b": nrm(ks[12], (L, CONV_WIDTH), 0.02),
        "w_out": nrm(ks[13], (L, MIX_WIDTH, D_MODEL), MIX_WIDTH ** -0.5 * DEEPNORM_BETA),
        "b_out": nrm(ks[14], (L, D_MODEL), 0.02),
        "ln2_g": gain(ks[15], D_MODEL),
        "ln2_b": nrm(ks[16], (L, D_MODEL), 0.02),
        "ffn2_w_gate": nrm(ks[17], (L, D_MODEL, D_FF), D_MODEL ** -0.5),
        "ffn2_w_up": nrm(ks[18], (L, D_MODEL, D_FF), D_MODEL ** -0.5),
        "ffn2_w_down": nrm(ks[19], (L, D_FF, D_MODEL), D_FF ** -0.5 * DEEPNORM_BETA),
        "ln3_g": gain(ks[20], D_MODEL),
        "ln3_b": nrm(ks[21], (L, D_MODEL), 0.02),
    }


def reference(x, ffn1_w_gate, ffn1_w_up, ffn1_w_down, ln1_g, ln1_b, w_in, b_in, attn_sinks,
              conv_dw_w, conv_dw_b, conv_ln_g, conv_ln_b, w_out, b_out, ln2_g, ln2_b,
              ffn2_w_gate, ffn2_w_up, ffn2_w_down, ln3_g, ln3_b):
    B, S = x.shape[0], x.shape[1]
    split_points = [ATTN_WIDTH, ATTN_WIDTH + KV_WIDTH, ATTN_WIDTH + 2 * KV_WIDTH]
    for l in range(DEPTH):
        x = layer_norm(DEEPNORM_ALPHA * x + 0.5 * swiglu_ffn(x, ffn1_w_gate[l], ffn1_w_up[l], ffn1_w_down[l]),
                       ln1_g[l], ln1_b[l])
        u = x @ w_in[l] + b_in[l]
        q, k, v, conv_in = jnp.split(u, split_points, axis=-1)
        q = q.reshape(B, S, N_Q_HEADS, HEAD_DIM)
        k = k.reshape(B, S, N_KV_HEADS, HEAD_DIM)
        v = v.reshape(B, S, N_KV_HEADS, HEAD_DIM)
        attn_out = sliding_window_attention_with_sinks(q, k, v, attn_sinks[l])
        conv_out = conformer_conv_group(conv_in, conv_dw_w[l], conv_dw_b[l],
                                        conv_ln_g[l], conv_ln_b[l])
        mixed = jnp.concatenate([attn_out, conv_out], axis=-1) @ w_out[l] + b_out[l]
        x = layer_norm(DEEPNORM_ALPHA * x + mixed, ln2_g[l], ln2_b[l])
        x = layer_norm(DEEPNORM_ALPHA * x + 0.5 * swiglu_ffn(x, ffn2_w_gate[l], ffn2_w_up[l], ffn2_w_down[l]),
                       ln3_g[l], ln3_b[l])
    return x
```

```python
import functools
import math

import jax
import jax.numpy as jnp
from jax import lax
from jax.experimental import pallas as pl
from jax.experimental.pallas import tpu as pltpu

D_MODEL = 2048
HEAD_DIM = 64
N_Q_HEADS = 16
N_KV_HEADS = 2
GQA_GROUP = N_Q_HEADS // N_KV_HEADS
ATTN_WIDTH = N_Q_HEADS * HEAD_DIM
KV_WIDTH = N_KV_HEADS * HEAD_DIM
QKV_WIDTH = ATTN_WIDTH + 2 * KV_WIDTH
WINDOW = 128
BLOCK = 128
CONV_WIDTH = D_MODEL - ATTN_WIDTH
CONV_KERNEL = 31
IN_WIDTH = QKV_WIDTH + 2 * CONV_WIDTH
D_FF = 5632
LN_EPS = 1e-5
DEPTH = 1
DEEPNORM_ALPHA = (2.0 * DEPTH) ** 0.25
ATTN_SCALE = 1.0 / math.sqrt(HEAD_DIM)

F32 = jnp.float32
BF16 = jnp.bfloat16

V7X_VMEM_BYTES = 64 * 1024 * 1024
VMEM_LIMIT_CAP = V7X_VMEM_BYTES - 6 * 1024 * 1024
SUBLANES = 8
LANES = 128

FFN_TM = 512
FFN_TF = 512
PROJ_TM = 512
LN_ROWS = 16
CONV_TS = 128
CONV_HALO = 32
CONV_ROWS = 32


def _nbytes(shape, dtype):
    return math.prod(shape) * jnp.dtype(dtype).itemsize


def _vmem_limit(pipelined, resident):
    need = 2 * sum(_nbytes(s, d) for s, d in pipelined) + sum(_nbytes(s, d) for s, d in resident)
    return int(min(VMEM_LIMIT_CAP, need + (4 << 20)))


def _layer_norm(y, g, b):
    mu = jnp.mean(y, axis=-1, keepdims=True)
    yc = y - mu
    var = jnp.mean(yc * yc, axis=-1, keepdims=True)
    return yc * lax.rsqrt(var + LN_EPS) * g + b


def _residual_ln_rows(x_ref, upd_ref, upd_scale, g_ref, b_ref, o_ref, rows):
    def body(i, carry):
        r = pl.multiple_of(i * LN_ROWS, LN_ROWS)
        y = DEEPNORM_ALPHA * x_ref[pl.ds(r, LN_ROWS), :] + upd_scale * upd_ref[pl.ds(r, LN_ROWS), :]
        o_ref[pl.ds(r, LN_ROWS), :] = _layer_norm(y, g_ref[...], b_ref[...])
        return carry
    lax.fori_loop(0, rows // LN_ROWS, body, 0)


def _ffn_ln_body(x_ref, wg_ref, wu_ref, wd_ref, g_ref, b_ref, o_ref, xb_ref, acc_ref, *, nf):
    f = pl.program_id(1)

    @pl.when(f == 0)
    def _():
        xb_ref[...] = x_ref[...].astype(BF16)
        acc_ref[...] = jnp.zeros_like(acc_ref)

    xb = xb_ref[...]
    gate = jnp.dot(xb, wg_ref[...], preferred_element_type=F32)
    up = jnp.dot(xb, wu_ref[...], preferred_element_type=F32)
    h = (gate * jax.nn.sigmoid(gate) * up).astype(BF16)
    acc_ref[...] += jnp.dot(h, wd_ref[...], preferred_element_type=F32)

    @pl.when(f == nf - 1)
    def _():
        _residual_ln_rows(x_ref, acc_ref, 0.5, g_ref, b_ref, o_ref, x_ref.shape[0])


def _ffn_ln(x, wg, wu, wd, g, b):
    m, d = x.shape
    dff = wg.shape[1]
    tm, tf = FFN_TM, FFN_TF
    nf = dff // tf
    assert m % tm == 0 and dff % tf == 0
    pipelined = [((tm, d), F32), ((d, tf), BF16), ((d, tf), BF16), ((tf, d), BF16), ((tm, d), F32)]
    resident = [((tm, d), BF16), ((tm, d), F32), ((tm, tf), F32), ((tm, tf), F32), ((tm, tf), F32),
                ((tm, d), F32)]
    return pl.pallas_call(
        functools.partial(_ffn_ln_body, nf=nf),
        grid=(m // tm, nf),
        in_specs=[
            pl.BlockSpec((tm, d), lambda i, f: (i, 0)),
            pl.BlockSpec((d, tf), lambda i, f: (0, f)),
            pl.BlockSpec((d, tf), lambda i, f: (0, f)),
            pl.BlockSpec((tf, d), lambda i, f: (f, 0)),
            pl.BlockSpec((1, d), lambda i, f: (0, 0)),
            pl.BlockSpec((1, d), lambda i, f: (0, 0)),
        ],
        out_specs=pl.BlockSpec((tm, d), lambda i, f: (i, 0)),
        out_shape=jax.ShapeDtypeStruct((m, d), F32),
        scratch_shapes=[pltpu.VMEM((tm, d), BF16), pltpu.VMEM((tm, d), F32)],
        compiler_params=pltpu.CompilerParams(
            dimension_semantics=("arbitrary", "arbitrary"),
            vmem_limit_bytes=_vmem_limit(pipelined, resident)),
        name="ffn_ln",
    )(x, wg, wu, wd, g, b)


def _in_proj_body(x_ref, w_ref, b_ref, qkv_ref, glu_ref):
    xb = x_ref[...].astype(BF16)
    a0, g0 = QKV_WIDTH, QKV_WIDTH + CONV_WIDTH
    qkv = jnp.dot(xb, w_ref[:, :a0], preferred_element_type=F32) + b_ref[:, :a0]
    qkv_ref[...] = qkv.astype(BF16)
    val = jnp.dot(xb, w_ref[:, a0:g0], preferred_element_type=F32) + b_ref[:, a0:g0]
    gate = jnp.dot(xb, w_ref[:, g0:], preferred_element_type=F32) + b_ref[:, g0:]
    glu_ref[...] = val * jax.nn.sigmoid(gate)


def _in_proj(x, w, b):
    m, d = x.shape
    n = w.shape[1]
    tm = PROJ_TM
    pipelined = [((tm, d), F32), ((d, n), BF16), ((tm, QKV_WIDTH), BF16), ((tm, CONV_WIDTH), F32)]
    resident = [((tm, d), BF16), ((tm, QKV_WIDTH), F32), ((tm, CONV_WIDTH), F32), ((tm, CONV_WIDTH), F32)]
    return pl.pallas_call(
        _in_proj_body,
        grid=(m // tm,),
        in_specs=[
            pl.BlockSpec((tm, d), lambda i: (i, 0)),
            pl.BlockSpec((d, n), lambda i: (0, 0)),
            pl.BlockSpec((1, n), lambda i: (0, 0)),
        ],
        out_specs=[
            pl.BlockSpec((tm, QKV_WIDTH), lambda i: (i, 0)),
            pl.BlockSpec((tm, CONV_WIDTH), lambda i: (i, 0)),
        ],
        out_shape=[jax.ShapeDtypeStruct((m, QKV_WIDTH), BF16),
                   jax.ShapeDtypeStruct((m, CONV_WIDTH), F32)],
        compiler_params=pltpu.CompilerParams(
            dimension_semantics=("arbitrary",),
            vmem_limit_bytes=_vmem_limit(pipelined, resident)),
        name="in_proj",
    )(x, w, b)


def _attn_body(sink_ref, q_ref, kvp_ref, kvc_ref, o_ref):
    i = pl.program_id(1)
    q = q_ref[0]
    kv = jnp.concatenate([kvp_ref[0], kvc_ref[0]], axis=0)
    row = lax.broadcasted_iota(jnp.int32, (BLOCK, 2 * BLOCK), 0)
    col = lax.broadcasted_iota(jnp.int32, (BLOCK, 2 * BLOCK), 1)
    delta = BLOCK + row - col
    valid = (delta >= 0) & (delta < WINDOW) & ((col >= BLOCK) | (i > 0))
    neg = jnp.finfo(F32).min
    outs = []
    for h in range(N_Q_HEADS):
        kvh = h // GQA_GROUP
        qh = q[:, h * HEAD_DIM:(h + 1) * HEAD_DIM]
        k = kv[:, kvh * HEAD_DIM:(kvh + 1) * HEAD_DIM]
        v = kv[:, KV_WIDTH + kvh * HEAD_DIM:KV_WIDTH + (kvh + 1) * HEAD_DIM]
        s = lax.dot_general(qh, k, (((1,), (1,)), ((), ())), preferred_element_type=F32) * ATTN_SCALE
        s = jnp.where(valid, s, neg)
        sink = sink_ref[h]
        mx = jnp.maximum(jnp.max(s, axis=-1, keepdims=True), sink)
        p = jnp.exp(s - mx)
        denom = jnp.sum(p, axis=-1, keepdims=True) + jnp.exp(sink - mx)
        o = jnp.dot(p.astype(BF16), v, preferred_element_type=F32) / denom
        outs.append(o)
    o_ref[0] = jnp.concatenate(outs, axis=-1).astype(BF16)


def _attention(qkv, sinks):
    b, s, _ = qkv.shape
    nb = s // BLOCK
    kv_col = ATTN_WIDTH // (2 * KV_WIDTH)
    assert ATTN_WIDTH % (2 * KV_WIDTH) == 0
    return pl.pallas_call(
        _attn_body,
        grid=(b, nb),
        in_specs=[
            pl.BlockSpec(memory_space=pltpu.SMEM),
            pl.BlockSpec((1, BLOCK, ATTN_WIDTH), lambda bi, i: (bi, i, 0)),
            pl.BlockSpec((1, BLOCK, 2 * KV_WIDTH), lambda bi, i: (bi, jnp.maximum(i - 1, 0), kv_col)),
            pl.BlockSpec((1, BLOCK, 2 * KV_WIDTH), lambda bi, i: (bi, i, kv_col)),
        ],
        out_specs=pl.BlockSpec((1, BLOCK, ATTN_WIDTH), lambda bi, i: (bi, i, 0)),
        out_shape=jax.ShapeDtypeStruct((b, s, ATTN_WIDTH), BF16),
        compiler_params=pltpu.CompilerParams(dimension_semantics=("arbitrary", "arbitrary")),
        name="swa_attention",
    )(sinks, qkv, qkv, qkv)


def _conv_body(hp_ref, hc_ref, w_ref, cb_ref, g_ref, b_ref, o_ref, buf_ref):
    i = pl.program_id(1)
    ts = hc_ref.shape[1]

    @pl.when(i == 0)
    def _():
        buf_ref[0:CONV_HALO, :] = jnp.zeros((CONV_HALO, CONV_WIDTH), F32)

    @pl.when(i > 0)
    def _():
        buf_ref[0:CONV_HALO, :] = hp_ref[0]

    buf_ref[CONV_HALO:CONV_HALO + ts, :] = hc_ref[0]

    first = CONV_HALO - (CONV_KERNEL - 1)
    for r0 in range(0, ts, CONV_ROWS):
        cols = []
        for c0 in range(0, CONV_WIDTH, LANES):
            acc = jnp.broadcast_to(cb_ref[:, c0:c0 + LANES], (CONV_ROWS, LANES))
            for j in range(CONV_KERNEL):
                tap = buf_ref[r0 + first + j:r0 + first + j + CONV_ROWS, c0:c0 + LANES]
                acc = acc + tap * w_ref[j:j + 1, c0:c0 + LANES]
            cols.append(acc)
        y = _layer_norm(jnp.concatenate(cols, axis=-1), g_ref[...], b_ref[...])
        o_ref[0, r0:r0 + CONV_ROWS, :] = (y * jax.nn.sigmoid(y)).astype(BF16)


def _conv_module(glu, w, cb, g, b):
    bsz, s, c = glu.shape
    ts = CONV_TS
    halo_blocks = ts // CONV_HALO
    return pl.pallas_call(
        _conv_body,
        grid=(bsz, s // ts),
        in_specs=[
            pl.BlockSpec((1, CONV_HALO, c), lambda bi, i: (bi, jnp.maximum(i * halo_blocks - 1, 0), 0)),
            pl.BlockSpec((1, ts, c), lambda bi, i: (bi, i, 0)),
            pl.BlockSpec((CONV_KERNEL, c), lambda bi, i: (0, 0)),
            pl.BlockSpec((1, c), lambda bi, i: (0, 0)),
            pl.BlockSpec((1, c), lambda bi, i: (0, 0)),
            pl.BlockSpec((1, c), lambda bi, i: (0, 0)),
        ],
        out_specs=pl.BlockSpec((1, ts, c), lambda bi, i: (bi, i, 0)),
        out_shape=jax.ShapeDtypeStruct((bsz, s, c), BF16),
        scratch_shapes=[pltpu.VMEM((CONV_HALO + ts, c), F32)],
        compiler_params=pltpu.CompilerParams(dimension_semantics=("arbitrary", "arbitrary")),
        name="conv_module",
    )(glu, glu, w, cb, g, b)


def _out_proj_body(a_ref, c_ref, x_ref, w_ref, bo_ref, g_ref, b_ref, o_ref, acc_ref):
    mixed = jnp.dot(a_ref[...], w_ref[:ATTN_WIDTH, :], preferred_element_type=F32)
    mixed = mixed + jnp.dot(c_ref[...], w_ref[ATTN_WIDTH:, :], preferred_element_type=F32)
    acc_ref[...] = mixed + bo_ref[...]
    _residual_ln_rows(x_ref, acc_ref, 1.0, g_ref, b_ref, o_ref, x_ref.shape[0])


def _out_proj_ln(attn, conv, x, w, bo, g, b):
    m, d = x.shape
    tm = PROJ_TM
    pipelined = [((tm, ATTN_WIDTH), BF16), ((tm, CONV_WIDTH), BF16), ((tm, d), F32), ((d, d), BF16),
                 ((tm, d), F32)]
    resident = [((tm, d), F32), ((tm, d), F32)]
    return pl.pallas_call(
        _out_proj_body,
        grid=(m // tm,),
        in_specs=[
            pl.BlockSpec((tm, ATTN_WIDTH), lambda i: (i, 0)),
            pl.BlockSpec((tm, CONV_WIDTH), lambda i: (i, 0)),
            pl.BlockSpec((tm, d), lambda i: (i, 0)),
            pl.BlockSpec((d, d), lambda i: (0, 0)),
            pl.BlockSpec((1, d), lambda i: (0, 0)),
            pl.BlockSpec((1, d), lambda i: (0, 0)),
            pl.BlockSpec((1, d), lambda i: (0, 0)),
        ],
        out_specs=pl.BlockSpec((tm, d), lambda i: (i, 0)),
        out_shape=jax.ShapeDtypeStruct((m, d), F32),
        scratch_shapes=[pltpu.VMEM((tm, d), F32)],
        compiler_params=pltpu.CompilerParams(
            dimension_semantics=("arbitrary",),
            vmem_limit_bytes=_vmem_limit(pipelined, resident)),
        name="out_proj_ln",
    )(attn, conv, x, w, bo, g, b)


def kernel(x, ffn1_w_gate, ffn1_w_up, ffn1_w_down, ln1_g, ln1_b, w_in, b_in, attn_sinks,
           conv_dw_w, conv_dw_b, conv_ln_g, conv_ln_b, w_out, b_out, ln2_g, ln2_b,
           ffn2_w_gate, ffn2_w_up, ffn2_w_down, ln3_g, ln3_b):
    bsz, s, d = x.shape
    m = bsz * s
    h = x.reshape(m, d)
    assert ffn1_w_gate.shape[0] == DEPTH
    for l in range(DEPTH):
        h = _ffn_ln(h, ffn1_w_gate[l].astype(BF16), ffn1_w_up[l].astype(BF16),
                    ffn1_w_down[l].astype(BF16), ln1_g[l][None], ln1_b[l][None])
        qkv, glu = _in_proj(h, w_in[l].astype(BF16), b_in[l][None])
        attn = _attention(qkv.reshape(bsz, s, QKV_WIDTH), attn_sinks[l])
        conv = _conv_module(glu.reshape(bsz, s, CONV_WIDTH), conv_dw_w[l], conv_dw_b[l][None],
                            conv_ln_g[l][None], conv_ln_b[l][None])
        h = _out_proj_ln(attn.reshape(m, ATTN_WIDTH), conv.reshape(m, CONV_WIDTH), h,
                         w_out[l].astype(BF16), b_out[l][None], ln2_g[l][None], ln2_b[l][None])
        h = _ffn_ln(h, ffn2_w_gate[l].astype(BF16), ffn2_w_up[l].astype(BF16),
                    ffn2_w_down[l].astype(BF16), ln3_g[l][None], ln3_b[l][None])
    return h.reshape(bsz, s, d)
```

```python
import functools
import math

import jax
import jax.numpy as jnp
from jax import lax
from jax.experimental import pallas as pl
from jax.experimental.pallas import tpu as pltpu

D_MODEL = 2048
HEAD_DIM = 64
N_Q_HEADS = 16
N_KV_HEADS = 2
GQA_GROUP = N_Q_HEADS // N_KV_HEADS
ATTN_WIDTH = N_Q_HEADS * HEAD_DIM
KV_WIDTH = N_KV_HEADS * HEAD_DIM
QKV_WIDTH = ATTN_WIDTH + 2 * KV_WIDTH
WINDOW = 128
BLOCK = 128
CONV_WIDTH = D_MODEL - ATTN_WIDTH
CONV_KERNEL = 31
IN_WIDTH = QKV_WIDTH + 2 * CONV_WIDTH
D_FF = 5632
LN_EPS = 1e-5
DEPTH = 1
DEEPNORM_ALPHA = (2.0 * DEPTH) ** 0.25
ATTN_SCALE = 1.0 / math.sqrt(HEAD_DIM)

F32 = jnp.float32
BF16 = jnp.bfloat16

V7X_VMEM_BYTES = 64 * 1024 * 1024
VMEM_LIMIT_CAP = V7X_VMEM_BYTES - 6 * 1024 * 1024
SUBLANES = 8
LANES = 128

FFN_TM = 1024
FFN_TF = 256
PROJ_TM = 512
LN_ROWS = 16
LN_UNROLL = 8
CONV_TS = 128
CONV_HALO = 32
CONV_ROWS = 32


def _nbytes(shape, dtype):
    return math.prod(shape) * jnp.dtype(dtype).itemsize


def _vmem_limit(pipelined, resident):
    need = 2 * sum(_nbytes(s, d) for s, d in pipelined) + sum(_nbytes(s, d) for s, d in resident)
    return int(min(VMEM_LIMIT_CAP, need + (4 << 20)))


def _layer_norm(y, g, b):
    mu = jnp.mean(y, axis=-1, keepdims=True)
    yc = y - mu
    var = jnp.mean(yc * yc, axis=-1, keepdims=True)
    return yc * lax.rsqrt(var + LN_EPS) * g + b


def _residual_ln_rows(x_ref, upd_ref, upd_scale, g_ref, b_ref, o_ref, rows):
    trip_rows = LN_ROWS * LN_UNROLL

    def body(i, carry):
        base = pl.multiple_of(i * trip_rows, trip_rows)
        chunks = [pl.ds(base + k * LN_ROWS, LN_ROWS) for k in range(LN_UNROLL)]
        ys = [DEEPNORM_ALPHA * x_ref[rows_k, :] + upd_scale * upd_ref[rows_k, :] for rows_k in chunks]
        for rows_k, y in zip(chunks, ys):
            o_ref[rows_k, :] = _layer_norm(y, g_ref[...], b_ref[...])
        return carry
    lax.fori_loop(0, rows // trip_rows, body, 0)


def _ffn_ln_body(x_ref, wg_ref, wu_ref, wd_ref, g_ref, b_ref, o_ref, xb_ref, *, nf):
    f = pl.program_id(1)

    @pl.when(f == 0)
    def _():
        xb_ref[...] = x_ref[...].astype(BF16)
        o_ref[...] = jnp.zeros_like(o_ref)

    xb = xb_ref[...]
    gate = jnp.dot(xb, wg_ref[...].astype(BF16), preferred_element_type=F32)
    up = jnp.dot(xb, wu_ref[...].astype(BF16), preferred_element_type=F32)
    h = (gate * jax.nn.sigmoid(gate) * up).astype(BF16)
    o_ref[...] += jnp.dot(h, wd_ref[...].astype(BF16), preferred_element_type=F32)

    @pl.when(f == nf - 1)
    def _():
        _residual_ln_rows(x_ref, o_ref, 0.5, g_ref, b_ref, o_ref, x_ref.shape[0])


def _ffn_ln(x, wg, wu, wd, g, b):
    m, d = x.shape
    dff = wg.shape[1]
    tm, tf = FFN_TM, FFN_TF
    nf = dff // tf
    assert m % tm == 0 and dff % tf == 0
    pipelined = [((tm, d), F32), ((d, tf), F32), ((d, tf), F32), ((tf, d), F32), ((tm, d), F32)]
    resident = [((tm, d), BF16), ((d, tf), BF16), ((d, tf), BF16), ((tf, d), BF16),
                ((tm, tf), F32), ((tm, tf), F32), ((tm, tf), BF16)]
    return pl.pallas_call(
        functools.partial(_ffn_ln_body, nf=nf),
        grid=(m // tm, nf),
        in_specs=[
            pl.BlockSpec((tm, d), lambda i, f: (i, 0)),
            pl.BlockSpec((d, tf), lambda i, f: (0, f)),
            pl.BlockSpec((d, tf), lambda i, f: (0, f)),
            pl.BlockSpec((tf, d), lambda i, f: (f, 0)),
            pl.BlockSpec((1, d), lambda i, f: (0, 0)),
            pl.BlockSpec((1, d), lambda i, f: (0, 0)),
        ],
        out_specs=pl.BlockSpec((tm, d), lambda i, f: (i, 0)),
        out_shape=jax.ShapeDtypeStruct((m, d), F32),
        scratch_shapes=[pltpu.VMEM((tm, d), BF16)],
        compiler_params=pltpu.CompilerParams(
            dimension_semantics=("arbitrary", "arbitrary"),
            vmem_limit_bytes=_vmem_limit(pipelined, resident)),
        name="ffn_ln",
    )(x, wg, wu, wd, g, b)


def _in_proj_body(x_ref, w_ref, b_ref, qkv_ref, glu_ref):
    xb = x_ref[...].astype(BF16)
    a0, g0 = QKV_WIDTH, QKV_WIDTH + CONV_WIDTH
    qkv = jnp.dot(xb, w_ref[:, :a0], preferred_element_type=F32) + b_ref[:, :a0]
    qkv_ref[...] = qkv.astype(BF16)
    val = jnp.dot(xb, w_ref[:, a0:g0], preferred_element_type=F32) + b_ref[:, a0:g0]
    gate = jnp.dot(xb, w_ref[:, g0:], preferred_element_type=F32) + b_ref[:, g0:]
    glu_ref[...] = val * jax.nn.sigmoid(gate)


def _in_proj(x, w, b):
    m, d = x.shape
    n = w.shape[1]
    tm = PROJ_TM
    pipelined = [((tm, d), F32), ((d, n), BF16), ((tm, QKV_WIDTH), BF16), ((tm, CONV_WIDTH), F32)]
    resident = [((tm, d), BF16), ((tm, QKV_WIDTH), F32), ((tm, CONV_WIDTH), F32), ((tm, CONV_WIDTH), F32)]
    return pl.pallas_call(
        _in_proj_body,
        grid=(m // tm,),
        in_specs=[
            pl.BlockSpec((tm, d), lambda i: (i, 0)),
            pl.BlockSpec((d, n), lambda i: (0, 0)),
            pl.BlockSpec((1, n), lambda i: (0, 0)),
        ],
        out_specs=[
            pl.BlockSpec((tm, QKV_WIDTH), lambda i: (i, 0)),
            pl.BlockSpec((tm, CONV_WIDTH), lambda i: (i, 0)),
        ],
        out_shape=[jax.ShapeDtypeStruct((m, QKV_WIDTH), BF16),
                   jax.ShapeDtypeStruct((m, CONV_WIDTH), F32)],
        compiler_params=pltpu.CompilerParams(
            dimension_semantics=("arbitrary",),
            vmem_limit_bytes=_vmem_limit(pipelined, resident)),
        name="in_proj",
    )(x, w, b)


def _attn_body(sink_ref, q_ref, kvp_ref, kvc_ref, o_ref):
    i = pl.program_id(1)
    q = q_ref[0]
    kv = jnp.concatenate([kvp_ref[0], kvc_ref[0]], axis=0)
    row = lax.broadcasted_iota(jnp.int32, (BLOCK, 2 * BLOCK), 0)
    col = lax.broadcasted_iota(jnp.int32, (BLOCK, 2 * BLOCK), 1)
    delta = BLOCK + row - col
    valid = (delta >= 0) & (delta < WINDOW) & ((col >= BLOCK) | (i > 0))
    neg = jnp.finfo(F32).min
    outs = []
    for h in range(N_Q_HEADS):
        kvh = h // GQA_GROUP
        qh = q[:, h * HEAD_DIM:(h + 1) * HEAD_DIM]
        k = kv[:, kvh * HEAD_DIM:(kvh + 1) * HEAD_DIM]
        v = kv[:, KV_WIDTH + kvh * HEAD_DIM:KV_WIDTH + (kvh + 1) * HEAD_DIM]
        s = lax.dot_general(qh, k, (((1,), (1,)), ((), ())), preferred_element_type=F32) * ATTN_SCALE
        s = jnp.where(valid, s, neg)
        sink = sink_ref[h]
        mx = jnp.maximum(jnp.max(s, axis=-1, keepdims=True), sink)
        p = jnp.exp(s - mx)
        denom = jnp.sum(p, axis=-1, keepdims=True) + jnp.exp(sink - mx)
        o = jnp.dot(p.astype(BF16), v, preferred_element_type=F32) / denom
        outs.append(o)
    o_ref[0] = jnp.concatenate(outs, axis=-1).astype(BF16)


def _attention(qkv, sinks):
    b, s, _ = qkv.shape
    nb = s // BLOCK
    kv_col = ATTN_WIDTH // (2 * KV_WIDTH)
    assert ATTN_WIDTH % (2 * KV_WIDTH) == 0
    return pl.pallas_call(
        _attn_body,
        grid=(b, nb),
        in_specs=[
            pl.BlockSpec(memory_space=pltpu.SMEM),
            pl.BlockSpec((1, BLOCK, ATTN_WIDTH), lambda bi, i: (bi, i, 0)),
            pl.BlockSpec((1, BLOCK, 2 * KV_WIDTH), lambda bi, i: (bi, jnp.maximum(i - 1, 0), kv_col)),
            pl.BlockSpec((1, BLOCK, 2 * KV_WIDTH), lambda bi, i: (bi, i, kv_col)),
        ],
        out_specs=pl.BlockSpec((1, BLOCK, ATTN_WIDTH), lambda bi, i: (bi, i, 0)),
        out_shape=jax.ShapeDtypeStruct((b, s, ATTN_WIDTH), BF16),
        compiler_params=pltpu.CompilerParams(dimension_semantics=("arbitrary", "arbitrary")),
        name="swa_attention",
    )(sinks, qkv, qkv, qkv)


def _conv_body(hp_ref, hc_ref, w_ref, cb_ref, g_ref, b_ref, o_ref, buf_ref):
    i = pl.program_id(1)
    ts = hc_ref.shape[1]

    @pl.when(i == 0)
    def _():
        buf_ref[0:CONV_HALO, :] = jnp.zeros((CONV_HALO, CONV_WIDTH), F32)

    @pl.when(i > 0)
    def _():
        buf_ref[0:CONV_HALO, :] = hp_ref[0]

    buf_ref[CONV_HALO:CONV_HALO + ts, :] = hc_ref[0]

    first = CONV_HALO - (CONV_KERNEL - 1)
    for r0 in range(0, ts, CONV_ROWS):
        cols = []
        for c0 in range(0, CONV_WIDTH, LANES):
            acc = jnp.broadcast_to(cb_ref[:, c0:c0 + LANES], (CONV_ROWS, LANES))
            for j in range(CONV_KERNEL):
                tap = buf_ref[r0 + first + j:r0 + first + j + CONV_ROWS, c0:c0 + LANES]
                acc = acc + tap * w_ref[j:j + 1, c0:c0 + LANES]
            cols.append(acc)
        y = _layer_norm(jnp.concatenate(cols, axis=-1), g_ref[...], b_ref[...])
        o_ref[0, r0:r0 + CONV_ROWS, :] = (y * jax.nn.sigmoid(y)).astype(BF16)


def _conv_module(glu, w, cb, g, b):
    bsz, s, c = glu.shape
    ts = CONV_TS
    halo_blocks = ts // CONV_HALO
    return pl.pallas_call(
        _conv_body,
        grid=(bsz, s // ts),
        in_specs=[
            pl.BlockSpec((1, CONV_HALO, c), lambda bi, i: (bi, jnp.maximum(i * halo_blocks - 1, 0), 0)),
            pl.BlockSpec((1, ts, c), lambda bi, i: (bi, i, 0)),
            pl.BlockSpec((CONV_KERNEL, c), lambda bi, i: (0, 0)),
            pl.BlockSpec((1, c), lambda bi, i: (0, 0)),
            pl.BlockSpec((1, c), lambda bi, i: (0, 0)),
            pl.BlockSpec((1, c), lambda bi, i: (0, 0)),
        ],
        out_specs=pl.BlockSpec((1, ts, c), lambda bi, i: (bi, i, 0)),
        out_shape=jax.ShapeDtypeStruct((bsz, s, c), BF16),
        scratch_shapes=[pltpu.VMEM((CONV_HALO + ts, c), F32)],
        compiler_params=pltpu.CompilerParams(dimension_semantics=("arbitrary", "arbitrary")),
        name="conv_module",
    )(glu, glu, w, cb, g, b)


def _out_proj_body(a_ref, c_ref, x_ref, w_ref, bo_ref, g_ref, b_ref, o_ref, acc_ref):
    mixed = jnp.dot(a_ref[...], w_ref[:ATTN_WIDTH, :], preferred_element_type=F32)
    mixed = mixed + jnp.dot(c_ref[...], w_ref[ATTN_WIDTH:, :], preferred_element_type=F32)
    acc_ref[...] = mixed + bo_ref[...]
    _residual_ln_rows(x_ref, acc_ref, 1.0, g_ref, b_ref, o_ref, x_ref.shape[0])


def _out_proj_ln(attn, conv, x, w, bo, g, b):
    m, d = x.shape
    tm = PROJ_TM
    pipelined = [((tm, ATTN_WIDTH), BF16), ((tm, CONV_WIDTH), BF16), ((tm, d), F32), ((d, d), BF16),
                 ((tm, d), F32)]
    resident = [((tm, d), F32), ((tm, d), F32)]
    return pl.pallas_call(
        _out_proj_body,
        grid=(m // tm,),
        in_specs=[
            pl.BlockSpec((tm, ATTN_WIDTH), lambda i: (i, 0)),
            pl.BlockSpec((tm, CONV_WIDTH), lambda i: (i, 0)),
            pl.BlockSpec((tm, d), lambda i: (i, 0)),
            pl.BlockSpec((d, d), lambda i: (0, 0)),
            pl.BlockSpec((1, d), lambda i: (0, 0)),
            pl.BlockSpec((1, d), lambda i: (0, 0)),
            pl.BlockSpec((1, d), lambda i: (0, 0)),
        ],
        out_specs=pl.BlockSpec((tm, d), lambda i: (i, 0)),
        out_shape=jax.ShapeDtypeStruct((m, d), F32),
        scratch_shapes=[pltpu.VMEM((tm, d), F32)],
        compiler_params=pltpu.CompilerParams(
            dimension_semantics=("arbitrary",),
            vmem_limit_bytes=_vmem_limit(pipelined, resident)),
        name="out_proj_ln",
    )(attn, conv, x, w, bo, g, b)


def kernel(x, ffn1_w_gate, ffn1_w_up, ffn1_w_down, ln1_g, ln1_b, w_in, b_in, attn_sinks,
           conv_dw_w, conv_dw_b, conv_ln_g, conv_ln_b, w_out, b_out, ln2_g, ln2_b,
           ffn2_w_gate, ffn2_w_up, ffn2_w_down, ln3_g, ln3_b):
    bsz, s, d = x.shape
    m = bsz * s
    h = x.reshape(m, d)
    assert ffn1_w_gate.shape[0] == DEPTH
    for l in range(DEPTH):
        h = _ffn_ln(h, ffn1_w_gate[l], ffn1_w_up[l], ffn1_w_down[l], ln1_g[l][None], ln1_b[l][None])
        qkv, glu = _in_proj(h, w_in[l].astype(BF16), b_in[l][None])
        attn = _attention(qkv.reshape(bsz, s, QKV_WIDTH), attn_sinks[l])
        conv = _conv_module(glu.reshape(bsz, s, CONV_WIDTH), conv_dw_w[l], conv_dw_b[l][None],
                            conv_ln_g[l][None], conv_ln_b[l][None])
        h = _out_proj_ln(attn.reshape(m, ATTN_WIDTH), conv.reshape(m, CONV_WIDTH), h,
                         w_out[l].astype(BF16), b_out[l][None], ln2_g[l][None], ln2_b[l][None])
        h = _ffn_ln(h, ffn2_w_gate[l], ffn2_w_up[l], ffn2_w_down[l], ln3_g[l][None], ln3_b[l][None])
    return h.reshape(bsz, s, d)
```

```python
import functools
import math

import jax
import jax.numpy as jnp
from jax import lax
from jax.experimental import pallas as pl
from jax.experimental.pallas import tpu as pltpu

D_MODEL = 2048
HEAD_DIM = 64
N_Q_HEADS = 16
N_KV_HEADS = 2
GQA_GROUP = N_Q_HEADS // N_KV_HEADS
ATTN_WIDTH = N_Q_HEADS * HEAD_DIM
KV_WIDTH = N_KV_HEADS * HEAD_DIM
QKV_WIDTH = ATTN_WIDTH + 2 * KV_WIDTH
WINDOW = 128
BLOCK = 128
CONV_WIDTH = D_MODEL - ATTN_WIDTH
CONV_KERNEL = 31
IN_WIDTH = QKV_WIDTH + 2 * CONV_WIDTH
D_FF = 5632
LN_EPS = 1e-5
DEPTH = 1
DEEPNORM_ALPHA = (2.0 * DEPTH) ** 0.25
ATTN_SCALE = 1.0 / math.sqrt(HEAD_DIM)

F32 = jnp.float32
BF16 = jnp.bfloat16

V7X_VMEM_BYTES = 64 * 1024 * 1024
VMEM_LIMIT_CAP = V7X_VMEM_BYTES - 6 * 1024 * 1024
SUBLANES = 8
LANES = 128

FFN_TM = 1024
FFN_TF = 256
PROJ_TM = 512
LN_ROWS = 16
LN_UNROLL = 8
CONV_TS = 256
CONV_HALO = 32
CONV_STRIDE = 4
CONV_SUB = SUBLANES * CONV_STRIDE
CONV_LN_ROWS = 16


def _nbytes(shape, dtype):
    return math.prod(shape) * jnp.dtype(dtype).itemsize


def _vmem_limit(pipelined, resident):
    need = 2 * sum(_nbytes(s, d) for s, d in pipelined) + sum(_nbytes(s, d) for s, d in resident)
    return int(min(VMEM_LIMIT_CAP, need + (4 << 20)))


def _layer_norm(y, g, b):
    mu = jnp.mean(y, axis=-1, keepdims=True)
    yc = y - mu
    var = jnp.mean(yc * yc, axis=-1, keepdims=True)
    return yc * lax.rsqrt(var + LN_EPS) * g + b


def _residual_ln_rows(x_ref, upd_ref, upd_scale, g_ref, b_ref, o_ref, rows):
    trip_rows = LN_ROWS * LN_UNROLL

    def body(i, carry):
        base = pl.multiple_of(i * trip_rows, trip_rows)
        chunks = [pl.ds(base + k * LN_ROWS, LN_ROWS) for k in range(LN_UNROLL)]
        ys = [DEEPNORM_ALPHA * x_ref[rows_k, :] + upd_scale * upd_ref[rows_k, :] for rows_k in chunks]
        for rows_k, y in zip(chunks, ys):
            o_ref[rows_k, :] = _layer_norm(y, g_ref[...], b_ref[...])
        return carry
    lax.fori_loop(0, rows // trip_rows, body, 0)


def _ffn_ln_body(x_ref, wg_ref, wu_ref, wd_ref, g_ref, b_ref, o_ref, xb_ref, *, nf):
    f = pl.program_id(1)

    @pl.when(f == 0)
    def _():
        xb_ref[...] = x_ref[...].astype(BF16)
        o_ref[...] = jnp.zeros_like(o_ref)

    xb = xb_ref[...]
    gate = jnp.dot(xb, wg_ref[...].astype(BF16), preferred_element_type=F32)
    up = jnp.dot(xb, wu_ref[...].astype(BF16), preferred_element_type=F32)
    h = (gate * jax.nn.sigmoid(gate) * up).astype(BF16)
    o_ref[...] += jnp.dot(h, wd_ref[...].astype(BF16), preferred_element_type=F32)

    @pl.when(f == nf - 1)
    def _():
        _residual_ln_rows(x_ref, o_ref, 0.5, g_ref, b_ref, o_ref, x_ref.shape[0])


def _ffn_ln(x, wg, wu, wd, g, b):
    m, d = x.shape
    dff = wg.shape[1]
    tm, tf = FFN_TM, FFN_TF
    nf = dff // tf
    assert m % tm == 0 and dff % tf == 0
    pipelined = [((tm, d), F32), ((d, tf), F32), ((d, tf), F32), ((tf, d), F32), ((tm, d), F32)]
    resident = [((tm, d), BF16), ((d, tf), BF16), ((d, tf), BF16), ((tf, d), BF16),
                ((tm, tf), F32), ((tm, tf), F32), ((tm, tf), BF16)]
    return pl.pallas_call(
        functools.partial(_ffn_ln_body, nf=nf),
        grid=(m // tm, nf),
        in_specs=[
            pl.BlockSpec((tm, d), lambda i, f: (i, 0)),
            pl.BlockSpec((d, tf), lambda i, f: (0, f)),
            pl.BlockSpec((d, tf), lambda i, f: (0, f)),
            pl.BlockSpec((tf, d), lambda i, f: (f, 0)),
            pl.BlockSpec((1, d), lambda i, f: (0, 0)),
            pl.BlockSpec((1, d), lambda i, f: (0, 0)),
        ],
        out_specs=pl.BlockSpec((tm, d), lambda i, f: (i, 0)),
        out_shape=jax.ShapeDtypeStruct((m, d), F32),
        scratch_shapes=[pltpu.VMEM((tm, d), BF16)],
        compiler_params=pltpu.CompilerParams(
            dimension_semantics=("arbitrary", "arbitrary"),
            vmem_limit_bytes=_vmem_limit(pipelined, resident)),
        name="ffn_ln",
    )(x, wg, wu, wd, g, b)


def _in_proj_body(x_ref, w_ref, b_ref, qkv_ref, glu_ref):
    xb = x_ref[...].astype(BF16)
    a0, g0 = QKV_WIDTH, QKV_WIDTH + CONV_WIDTH
    qkv = jnp.dot(xb, w_ref[:, :a0], preferred_element_type=F32) + b_ref[:, :a0]
    qkv_ref[...] = qkv.astype(BF16)
    val = jnp.dot(xb, w_ref[:, a0:g0], preferred_element_type=F32) + b_ref[:, a0:g0]
    gate = jnp.dot(xb, w_ref[:, g0:], preferred_element_type=F32) + b_ref[:, g0:]
    glu_ref[...] = val * jax.nn.sigmoid(gate)


def _in_proj(x, w, b):
    m, d = x.shape
    n = w.shape[1]
    tm = PROJ_TM
    pipelined = [((tm, d), F32), ((d, n), BF16), ((tm, QKV_WIDTH), BF16), ((tm, CONV_WIDTH), F32)]
    resident = [((tm, d), BF16), ((tm, QKV_WIDTH), F32), ((tm, CONV_WIDTH), F32), ((tm, CONV_WIDTH), F32)]
    return pl.pallas_call(
        _in_proj_body,
        grid=(m // tm,),
        in_specs=[
            pl.BlockSpec((tm, d), lambda i: (i, 0)),
            pl.BlockSpec((d, n), lambda i: (0, 0)),
            pl.BlockSpec((1, n), lambda i: (0, 0)),
        ],
        out_specs=[
            pl.BlockSpec((tm, QKV_WIDTH), lambda i: (i, 0)),
            pl.BlockSpec((tm, CONV_WIDTH), lambda i: (i, 0)),
        ],
        out_shape=[jax.ShapeDtypeStruct((m, QKV_WIDTH), BF16),
                   jax.ShapeDtypeStruct((m, CONV_WIDTH), F32)],
        compiler_params=pltpu.CompilerParams(
            dimension_semantics=("arbitrary",),
            vmem_limit_bytes=_vmem_limit(pipelined, resident)),
        name="in_proj",
    )(x, w, b)


def _attn_body(sink_ref, q_ref, kvp_ref, kvc_ref, o_ref):
    i = pl.program_id(1)
    kv = jnp.concatenate([kvp_ref[0], kvc_ref[0]], axis=0)
    k_all, v_all = kv[:, :KV_WIDTH], kv[:, KV_WIDTH:]
    k_swap = pltpu.roll(k_all, HEAD_DIM, axis=1)
    v_swap = pltpu.roll(v_all, HEAD_DIM, axis=1)
    low = lax.broadcasted_iota(jnp.int32, (2 * BLOCK, KV_WIDTH), 1) < HEAD_DIM
    zero = jnp.zeros_like(k_all)

    assert WINDOW == BLOCK
    row = lax.broadcasted_iota(jnp.int32, (BLOCK, BLOCK), 0)
    col = lax.broadcasted_iota(jnp.int32, (BLOCK, BLOCK), 1)
    from_prev = col > row
    exists = jnp.logical_not(from_prev) | (i > 0)
    neg = jnp.finfo(F32).min
    zero_p = jnp.zeros((BLOCK, BLOCK), BF16)
    out_low = lax.broadcasted_iota(jnp.int32, (BLOCK, 2 * HEAD_DIM), 1) < HEAD_DIM
    pairs = GQA_GROUP // 2

    for kvh in range(N_KV_HEADS):
        own, other = (k_all, k_swap) if kvh == 0 else (k_swap, k_all)
        k_lo, k_hi = jnp.where(low, own, zero), jnp.where(low, zero, other)
        own, other = (v_all, v_swap) if kvh == 0 else (v_swap, v_all)
        v_lo, v_hi = jnp.where(low, own, zero), jnp.where(low, zero, other)
        q_stack = jnp.concatenate(
            [q_ref[0, :, (kvh * pairs + p) * 2 * HEAD_DIM:(kvh * pairs + p + 1) * 2 * HEAD_DIM]
             for p in range(pairs)], axis=0)
        contract_lanes = (((1,), (1,)), ((), ()))
        s_even = lax.dot_general(q_stack, k_lo, contract_lanes, preferred_element_type=F32)
        s_odd = lax.dot_general(q_stack, k_hi, contract_lanes, preferred_element_type=F32)
        probs, denoms = ([], []), ([], [])
        for p in range(pairs):
            for parity, s_all in enumerate((s_even, s_odd)):
                sink = sink_ref[(kvh * pairs + p) * 2 + parity]
                s = s_all[p * BLOCK:(p + 1) * BLOCK, :]
                s = jnp.where(from_prev, s[:, :BLOCK], s[:, BLOCK:])
                s = jnp.where(exists, s * ATTN_SCALE, neg)
                mx = jnp.maximum(jnp.max(s, axis=-1, keepdims=True), sink)
                e = jnp.exp(s - mx)
                denoms[parity].append(jnp.sum(e, axis=-1, keepdims=True) + jnp.exp(sink - mx))
                e = e.astype(BF16)
                probs[parity].append(jnp.concatenate(
                    [jnp.where(from_prev, e, zero_p), jnp.where(from_prev, zero_p, e)], axis=1))
        o_stack = jnp.dot(jnp.concatenate(probs[0], axis=0), v_lo, preferred_element_type=F32)
        o_stack = o_stack + jnp.dot(jnp.concatenate(probs[1], axis=0), v_hi, preferred_element_type=F32)
        for p in range(pairs):
            den = jnp.where(out_low, denoms[0][p], denoms[1][p])
            o_pair = o_stack[p * BLOCK:(p + 1) * BLOCK, :] / den
            c0 = (kvh * pairs + p) * 2 * HEAD_DIM
            o_ref[0, :, c0:c0 + 2 * HEAD_DIM] = o_pair.astype(BF16)


def _attention(qkv, sinks):
    b, s, _ = qkv.shape
    nb = s // BLOCK
    kv_col = ATTN_WIDTH // (2 * KV_WIDTH)
    assert ATTN_WIDTH % (2 * KV_WIDTH) == 0
    return pl.pallas_call(
        _attn_body,
        grid=(b, nb),
        in_specs=[
            pl.BlockSpec(memory_space=pltpu.SMEM),
            pl.BlockSpec((1, BLOCK, ATTN_WIDTH), lambda bi, i: (bi, i, 0)),
            pl.BlockSpec((1, BLOCK, 2 * KV_WIDTH), lambda bi, i: (bi, jnp.maximum(i - 1, 0), kv_col)),
            pl.BlockSpec((1, BLOCK, 2 * KV_WIDTH), lambda bi, i: (bi, i, kv_col)),
        ],
        out_specs=pl.BlockSpec((1, BLOCK, ATTN_WIDTH), lambda bi, i: (bi, i, 0)),
        out_shape=jax.ShapeDtypeStruct((b, s, ATTN_WIDTH), BF16),
        compiler_params=pltpu.CompilerParams(dimension_semantics=("arbitrary", "arbitrary")),
        name="swa_attention",
    )(sinks, qkv, qkv, qkv)


def _conv_body(hp_ref, hc_ref, w_ref, cb_ref, g_ref, b_ref, o_ref, buf_ref, obuf_ref):
    i = pl.program_id(1)
    ts = hc_ref.shape[1]
    nslab = CONV_WIDTH // LANES
    slabs = [slice(c * LANES, (c + 1) * LANES) for c in range(nslab)]

    @pl.when(i == 0)
    def _():
        buf_ref[:, 0:CONV_HALO, :] = jnp.zeros((nslab, CONV_HALO, LANES), F32)

    @pl.when(i > 0)
    def _():
        for c, cs in enumerate(slabs):
            buf_ref[c, 0:CONV_HALO, :] = hp_ref[0, :, cs]

    for c, cs in enumerate(slabs):
        buf_ref[c, CONV_HALO:CONV_HALO + ts, :] = hc_ref[0, :, cs]

    first = CONV_HALO - (CONV_KERNEL - 1)

    def conv_sub_block(sb, carry):
        base = pl.multiple_of(sb * CONV_SUB, CONV_SUB)
        for c, cs in enumerate(slabs):
            acc = [jnp.broadcast_to(cb_ref[:, cs], (SUBLANES, LANES)) for _ in range(CONV_STRIDE)]
            for j in range(CONV_KERNEL):
                wj = w_ref[j:j + 1, cs]
                for p in range(CONV_STRIDE):
                    tap = buf_ref[c, pl.ds(base + (first + j + p), SUBLANES, stride=CONV_STRIDE), :]
                    acc[p] = acc[p] + tap * wj
            for p in range(CONV_STRIDE):
                obuf_ref[c, pl.ds(base + p, SUBLANES, stride=CONV_STRIDE), :] = acc[p]
        return carry

    lax.fori_loop(0, ts // CONV_SUB, conv_sub_block, 0)

    trip_rows = CONV_LN_ROWS * LN_UNROLL

    def ln_swish_rows(t, carry):
        base = pl.multiple_of(t * trip_rows, trip_rows)
        for k in range(LN_UNROLL):
            rows_k = pl.ds(base + k * CONV_LN_ROWS, CONV_LN_ROWS)
            y = jnp.concatenate([obuf_ref[c, rows_k, :] for c in range(nslab)], axis=-1)
            y = _layer_norm(y, g_ref[...], b_ref[...])
            o_ref[0, rows_k, :] = (y * jax.nn.sigmoid(y)).astype(BF16)
        return carry

    lax.fori_loop(0, ts // trip_rows, ln_swish_rows, 0)


def _conv_module(glu, w, cb, g, b):
    bsz, s, c = glu.shape
    ts = CONV_TS
    halo_blocks = ts // CONV_HALO
    return pl.pallas_call(
        _conv_body,
        grid=(bsz, s // ts),
        in_specs=[
            pl.BlockSpec((1, CONV_HALO, c), lambda bi, i: (bi, jnp.maximum(i * halo_blocks - 1, 0), 0)),
            pl.BlockSpec((1, ts, c), lambda bi, i: (bi, i, 0)),
            pl.BlockSpec((CONV_KERNEL, c), lambda bi, i: (0, 0)),
            pl.BlockSpec((1, c), lambda bi, i: (0, 0)),
            pl.BlockSpec((1, c), lambda bi, i: (0, 0)),
            pl.BlockSpec((1, c), lambda bi, i: (0, 0)),
        ],
        out_specs=pl.BlockSpec((1, ts, c), lambda bi, i: (bi, i, 0)),
        out_shape=jax.ShapeDtypeStruct((bsz, s, c), BF16),
        scratch_shapes=[pltpu.VMEM((c // LANES, CONV_HALO + ts, LANES), F32),
                        pltpu.VMEM((c // LANES, ts, LANES), F32)],
        compiler_params=pltpu.CompilerParams(dimension_semantics=("arbitrary", "arbitrary")),
        name="conv_module",
    )(glu, glu, w, cb, g, b)


def _out_proj_body(a_ref, c_ref, x_ref, w_ref, bo_ref, g_ref, b_ref, o_ref, acc_ref):
    mixed = jnp.dot(a_ref[...], w_ref[:ATTN_WIDTH, :], preferred_element_type=F32)
    mixed = mixed + jnp.dot(c_ref[...], w_ref[ATTN_WIDTH:, :], preferred_element_type=F32)
    acc_ref[...] = mixed + bo_ref[...]
    _residual_ln_rows(x_ref, acc_ref, 1.0, g_ref, b_ref, o_ref, x_ref.shape[0])


def _out_proj_ln(attn, conv, x, w, bo, g, b):
    m, d = x.shape
    tm = PROJ_TM
    pipelined = [((tm, ATTN_WIDTH), BF16), ((tm, CONV_WIDTH), BF16), ((tm, d), F32), ((d, d), BF16),
                 ((tm, d), F32)]
    resident = [((tm, d), F32), ((tm, d), F32)]
    return pl.pallas_call(
        _out_proj_body,
        grid=(m // tm,),
        in_specs=[
            pl.BlockSpec((tm, ATTN_WIDTH), lambda i: (i, 0)),
            pl.BlockSpec((tm, CONV_WIDTH), lambda i: (i, 0)),
            pl.BlockSpec((tm, d), lambda i: (i, 0)),
            pl.BlockSpec((d, d), lambda i: (0, 0)),
            pl.BlockSpec((1, d), lambda i: (0, 0)),
            pl.BlockSpec((1, d), lambda i: (0, 0)),
            pl.BlockSpec((1, d), lambda i: (0, 0)),
        ],
        out_specs=pl.BlockSpec((tm, d), lambda i: (i, 0)),
        out_shape=jax.ShapeDtypeStruct((m, d), F32),
        scratch_shapes=[pltpu.VMEM((tm, d), F32)],
        compiler_params=pltpu.CompilerParams(
            dimension_semantics=("arbitrary",),
            vmem_limit_bytes=_vmem_limit(pipelined, resident)),
        name="out_proj_ln",
    )(attn, conv, x, w, bo, g, b)


def kernel(x, ffn1_w_gate, ffn1_w_up, ffn1_w_down, ln1_g, ln1_b, w_in, b_in, attn_sinks,
           conv_dw_w, conv_dw_b, conv_ln_g, conv_ln_b, w_out, b_out, ln2_g, ln2_b,
           ffn2_w_gate, ffn2_w_up, ffn2_w_down, ln3_g, ln3_b):
    bsz, s, d = x.shape
    m = bsz * s
    h = x.reshape(m, d)
    assert ffn1_w_gate.shape[0] == DEPTH
    for l in range(DEPTH):
        h = _ffn_ln(h, ffn1_w_gate[l], ffn1_w_up[l], ffn1_w_down[l], ln1_g[l][None], ln1_b[l][None])
        qkv, glu = _in_proj(h, w_in[l].astype(BF16), b_in[l][None])
        attn = _attention(qkv.reshape(bsz, s, QKV_WIDTH), attn_sinks[l])
        conv = _conv_module(glu.reshape(bsz, s, CONV_WIDTH), conv_dw_w[l], conv_dw_b[l][None],
                            conv_ln_g[l][None], conv_ln_b[l][None])
        h = _out_proj_ln(attn.reshape(m, ATTN_WIDTH), conv.reshape(m, CONV_WIDTH), h,
                         w_out[l].astype(BF16), b_out[l][None], ln2_g[l][None], ln2_b[l][None])
        h = _ffn_ln(h, ffn2_w_gate[l], ffn2_w_up[l], ffn2_w_down[l], ln3_g[l][None], ln3_b[l][None])
    return h.reshape(bsz, s, d)
```
